```python
import math
import jax, jax.numpy as jnp
from jax import lax
import numpy as np

D_MODEL = 1024
BATCH = 2
SEQ = 16384
DEPTH = 2

GRID_W = 64
CTX_LEN = 256

NA_HEADS = 8
NA_HEAD_DIM = 64
NA_DIM = NA_HEADS * NA_HEAD_DIM
NA_WIN_ROWS = 8
NA_WIN_COLS = 16
GM_GROUPS = 4
GM_GROUP_DIM = 128
GM_WIDTH = GM_GROUPS * GM_GROUP_DIM
GM_CHUNK = 128
AB_QKV = 3 * NA_DIM
AB_IN = AB_QKV + 2 * GM_WIDTH
AB_MIX = NA_DIM + GM_WIDTH
HY_WIDTH = D_MODEL
HY_SHORT = 3
HY_BANDS = 16
HY_EMB = 1 + 2 * HY_BANDS
HY_FFN = 64
HY_DECAY_TARGET = 1e-2
HY_FAST_PCT = 0.3
HY_SLOW_PCT = 1.5
PEER_HEADS = 8
PEER_NKEYS = 128
PEER_EXPERTS = PEER_NKEYS * PEER_NKEYS
PEER_DK = 256
PEER_TOPK = 16
PEER_BLOCK = 128

N_AB = (DEPTH + 1) // 2
N_C = DEPTH // 2
DN_ALPHA = (2 * DEPTH) ** 0.25
DN_BETA = (8 * DEPTH) ** -0.25
LN_EPS = 1e-5
F32 = jnp.float32

kernel_name = 'hybrid_natten_gmlp_hyena_peer_dit'


def normalize(x):
    xf = x.astype(F32)
    mu = jnp.mean(xf, axis=-1, keepdims=True)
    var = jnp.mean(jnp.square(xf - mu), axis=-1, keepdims=True)
    return ((xf - mu) * lax.rsqrt(var + LN_EPS)).astype(x.dtype)


def layer_norm(x, g, b):
    return normalize(x) * g + b


def modulate(x, shift, scale):
    return x * (1 + scale) + shift


def split_heads(t):
    return t.reshape(t.shape[0], t.shape[1], NA_HEADS, NA_HEAD_DIM)


def neighbourhood_attention(q, k, v, k_ctx, v_ctx, rpb):
    B, L, H, dh = q.shape
    rows = L // GRID_W
    wr = min(NA_WIN_ROWS, rows)
    wc = NA_WIN_COLS
    qg = q.reshape(B, rows, GRID_W, H, dh)
    kg = k.reshape(B, rows, GRID_W, H, dh)
    vg = v.reshape(B, rows, GRID_W, H, dh)
    cols = jnp.arange(GRID_W)
    col_start = jnp.clip(cols - wc // 2, 0, GRID_W - wc)
    col_idx = col_start[:, None] + jnp.arange(wc)[None, :]
    col_rel = col_idx - cols[:, None] + (NA_WIN_COLS - 1)
    scale = dh ** -0.5

    def row_block(r):
        rs = jnp.clip(r - wr // 2, 0, rows - wr)
        q_r = lax.dynamic_index_in_dim(qg, r, axis=1, keepdims=False)
        k_r = lax.dynamic_slice_in_dim(kg, rs, wr, axis=1)[:, :, col_idx]
        v_r = lax.dynamic_slice_in_dim(vg, rs, wr, axis=1)[:, :, col_idx]
        row_rel = rs + jnp.arange(wr) - r + (NA_WIN_ROWS - 1)
        bias = rpb[:, row_rel][:, :, col_rel].transpose(0, 2, 1, 3)
        s_loc = jnp.einsum('bqhd,brqwhd->bhqrw', q_r, k_r).astype(F32) * scale + bias[None].astype(F32)
        s_ctx = jnp.einsum('bqhd,bkhd->bhqk', q_r, k_ctx).astype(F32) * scale
        logits = jnp.concatenate([s_loc.reshape(B, H, GRID_W, wr * wc), s_ctx], axis=-1)
        p = jax.nn.softmax(logits, axis=-1).astype(v.dtype)
        p_loc = p[..., :wr * wc].reshape(B, H, GRID_W, wr, wc)
        p_ctx = p[..., wr * wc:]
        return (jnp.einsum('bhqrw,brqwhd->bqhd', p_loc, v_r)
                + jnp.einsum('bhqk,bkhd->bqhd', p_ctx, v_ctx))

    out = lax.map(row_block, jnp.arange(rows))
    return out.transpose(1, 0, 2, 3, 4).reshape(B, L, H * dh)


def context_attention(q, k, v):
    B, Lc, H, dh = q.shape
    s = jnp.einsum('bqhd,bkhd->bhqk', q, k).astype(F32) * dh ** -0.5
    p = jax.nn.softmax(s, axis=-1).astype(v.dtype)
    return jnp.einsum('bhqk,bkhd->bqhd', p, v).reshape(B, Lc, H * dh)


def chunk_spatial_gating(u, gv, w_s, b_s):
    B, L, _ = u.shape
    n = L // GM_CHUNK
    gvc = normalize(gv).reshape(B, n, GM_CHUNK, GM_GROUPS, GM_GROUP_DIM)
    mixed = jnp.einsum('gpq,bnqgc->bnpgc', w_s, gvc) + b_s.T[None, None, :, :, None]
    return u * mixed.reshape(B, L, GM_WIDTH)


def ab_project(h, w_in):
    p = h @ w_in
    q, k, v = [split_heads(t) for t in jnp.split(p[..., :AB_QKV], 3, axis=-1)]
    u, gv = jnp.split(jax.nn.gelu(p[..., AB_QKV:], approximate=False), 2, axis=-1)
    return q, k, v, u, gv


def mixer_ab(h, h_ctx, w_in, w_out, rpb, w_s, b_s, ctx_out):
    q, k, v, u, gv = ab_project(h, w_in)
    if ctx_out:
        qc, kc, vc, uc, gvc = ab_project(h_ctx, w_in)
    else:
        kc, vc = [split_heads(t) for t in jnp.split(h_ctx @ w_in[:, NA_DIM:AB_QKV], 2, axis=-1)]
    a = neighbourhood_attention(q, k, v, kc, vc, rpb)
    g = chunk_spatial_gating(u, gv, w_s, b_s)
    y = jnp.concatenate([a, g], axis=-1) @ w_out
    y_ctx = None
    if ctx_out:
        ac = context_attention(qc, kc, vc)
        gc = chunk_spatial_gating(uc, gvc, w_s, b_s)
        y_ctx = jnp.concatenate([ac, gc], axis=-1) @ w_out
    return y, y_ctx


def hyena_filter(L, w1, b1, f1, w2, b2, f2, w3):
    w1, b1, f1, w2, b2, f2, w3 = [t.astype(F32) for t in (w1, b1, f1, w2, b2, f2, w3)]
    t = jnp.linspace(0.0, 1.0, L, dtype=F32)[:, None]
    ang = 2.0 * math.pi * jnp.arange(L, dtype=F32)[:, None] / L
    bands = jnp.linspace(1e-4, HY_BANDS - 1, HY_BANDS, dtype=F32)[None, :]
    z = jnp.concatenate([t, jnp.cos(bands * ang), jnp.sin(-bands * ang)], axis=-1)
    hid = jnp.sin(f1 * (z @ w1 + b1))
    hid = jnp.sin(f2 * (hid @ w2 + b2))
    filt = hid @ w3
    deltas = jnp.abs(jnp.linspace(math.log(HY_DECAY_TARGET) / HY_SLOW_PCT,
                                  math.log(HY_DECAY_TARGET) / HY_FAST_PCT, HY_WIDTH, dtype=F32))
    decay = jnp.exp(-t * deltas[None, :])
    fwd = filt[:, :HY_WIDTH] * decay
    bwd = filt[:, HY_WIDTH:] * decay
    l1 = jnp.sum(jnp.abs(fwd), axis=0) + jnp.sum(jnp.abs(bwd[1:]), axis=0)
    k2 = jnp.concatenate([fwd, jnp.zeros((1, HY_WIDTH), F32), bwd[:0:-1]], axis=0)
    return k2 / l1[None, :]


def long_conv(v, k2):
    L = v.shape[1]
    vf = jnp.fft.rfft(v.astype(F32), n=2 * L, axis=1)
    kf = jnp.fft.rfft(k2, n=2 * L, axis=0)
    y = jnp.fft.irfft(vf * kf[None], n=2 * L, axis=1)[:, :L]
    return y.astype(v.dtype)


def hyena(h, w_in, b_in, conv_w, conv_b, w1, b1, f1, w2, b2, f2, w3, d_skip, w_out):
    L = h.shape[1]
    z = h @ w_in + b_in
    z = lax.conv_general_dilated(z, conv_w[:, None, :], window_strides=(1,), padding=[(1, 1)],
                                 dimension_numbers=('NWC', 'WIO', 'NWC'),
                                 feature_group_count=3 * HY_WIDTH) + conv_b
    x0, x1, v = jnp.split(z, 3, axis=-1)
    k2 = hyena_filter(L, w1, b1, f1, w2, b2, f2, w3)
    v = v * x1
    y = (long_conv(v, k2) + v * d_skip) * x0
    return y @ w_out


def peer(h, wq, keys, u_tab, v_tab):
    B, L, D = h.shape
    n_tok = B * L
    xt = h.reshape(n_tok, D)
    q = (xt @ wq).reshape(n_tok, PEER_HEADS, 2, PEER_DK // 2)
    s = jnp.einsum('nhpd,hpkd->nhpk', q, keys).astype(F32)
    s1, i1 = lax.top_k(s[:, :, 0], PEER_TOPK)
    s2, i2 = lax.top_k(s[:, :, 1], PEER_TOPK)
    cand_s = (s1[..., :, None] + s2[..., None, :]).reshape(n_tok, PEER_HEADS, PEER_TOPK * PEER_TOPK)
    cand_i = (i1[..., :, None] * PEER_NKEYS + i2[..., None, :]).reshape(n_tok, PEER_HEADS, PEER_TOPK * PEER_TOPK)
    top_s, pos = lax.top_k(cand_s, PEER_TOPK)
    idx = jnp.take_along_axis(cand_i, pos, axis=-1)
    gate = jax.nn.softmax(top_s, axis=-1)
    nblk = n_tok // PEER_BLOCK

    def block(args):
        xb, ib, gb = args
        act = jax.nn.gelu(jnp.einsum('nd,nhkd->nhk', xb, u_tab[ib]).astype(F32), approximate=False)
        w = (gb * act).astype(xb.dtype)
        return jnp.einsum('nhk,nhkd->nd', w, v_tab[ib])

    out = lax.map(block, (xt.reshape(nblk, PEER_BLOCK, D),
                          idx.reshape(nblk, PEER_BLOCK, PEER_HEADS, PEER_TOPK),
                          gate.reshape(nblk, PEER_BLOCK, PEER_HEADS, PEER_TOPK)))
    return out.reshape(B, L, D)


def setup_inputs(seed: int = 0) -> dict:
    key = jax.random.key(seed)
    ks = iter(jax.random.split(key, 40))
    D = D_MODEL

    def nrm(shape, s):
        return jax.random.normal(next(ks), shape, F32) * s

    return {
        'x': nrm((BATCH, SEQ, D), 1.0),
        'c': nrm((BATCH, D), 1.0),
        'ctx': nrm((BATCH, CTX_LEN, D), 1.0),
        'c_ctx': nrm((D,), 1.0),
        'mod_w': nrm((DEPTH, D, 6 * D), D ** -0.5),
        'mod_b': nrm((DEPTH, 6 * D), 0.02),
        'ln_g': 1.0 + nrm((DEPTH, 2, D), 0.02),
        'ln_b': nrm((DEPTH, 2, D), 0.02),
        'ab_w_in': nrm((N_AB, D, AB_IN), D ** -0.5),
        'ab_w_out': nrm((N_AB, AB_MIX, D), DN_BETA * AB_MIX ** -0.5),
        'na_rpb': nrm((N_AB, NA_HEADS, 2 * NA_WIN_ROWS - 1, 2 * NA_WIN_COLS - 1), 0.5),
        'gm_w_s': nrm((N_AB, GM_GROUPS, GM_CHUNK, GM_CHUNK), GM_CHUNK ** -0.5),
        'gm_b_s': 1.0 + nrm((N_AB, GM_GROUPS, GM_CHUNK), 0.02),
        'hy_w_in': nrm((N_C, D, 3 * HY_WIDTH), D ** -0.5),
        'hy_b_in': nrm((N_C, 3 * HY_WIDTH), 0.02),
        'hy_conv_w': nrm((N_C, HY_SHORT, 3 * HY_WIDTH), HY_SHORT ** -0.5),
        'hy_conv_b': nrm((N_C, 3 * HY_WIDTH), 0.02),
        'hy_ffn_w1': nrm((N_C, HY_EMB, HY_FFN), HY_EMB ** -0.5),
        'hy_ffn_b1': nrm((N_C, HY_FFN), 0.02),
        'hy_ffn_f1': 1.0 + nrm((N_C, HY_FFN), 0.02),
        'hy_ffn_w2': nrm((N_C, HY_FFN, HY_FFN), HY_FFN ** -0.5),
        'hy_ffn_b2': nrm((N_C, HY_FFN), 0.02),
        'hy_ffn_f2': 1.0 + nrm((N_C, HY_FFN), 0.02),
        'hy_ffn_w3': nrm((N_C, HY_FFN, 2 * HY_WIDTH), HY_FFN ** -0.5),
        'hy_d': nrm((N_C, HY_WIDTH), 0.5),
        'hy_w_out': nrm((N_C, HY_WIDTH, D), DN_BETA * HY_WIDTH ** -0.5),
        'peer_wq': nrm((DEPTH, D, PEER_HEADS * PEER_DK), D ** -0.5),
        'peer_keys': nrm((DEPTH, PEER_HEADS, 2, PEER_NKEYS, PEER_DK // 2), (PEER_DK // 2) ** -0.5),
        'peer_u': nrm((DEPTH, PEER_EXPERTS, D), D ** -0.5),
        'peer_v': nrm((DEPTH, PEER_EXPERTS, D), DN_BETA),
    }


def reference(x, c, ctx, c_ctx, mod_w, mod_b, ln_g, ln_b, ab_w_in, ab_w_out, na_rpb, gm_w_s, gm_b_s,
              hy_w_in, hy_b_in, hy_conv_w, hy_conv_b, hy_ffn_w1, hy_ffn_b1, hy_ffn_f1, hy_ffn_w2,
              hy_ffn_b2, hy_ffn_f2, hy_ffn_w3, hy_d, hy_w_out, peer_wq, peer_keys, peer_u, peer_v):
    D = D_MODEL
    for i in range(DEPTH):
        j = i // 2
        ctx_out = any(m % 2 == 0 for m in range(i + 1, DEPTH))
        mod = (jax.nn.silu(c) @ mod_w[i] + mod_b[i])[:, None, :]
        sh1, sc1, g1, sh2, sc2, g2 = jnp.split(mod, 6, axis=-1)
        h = modulate(x, sh1, sc1)
        if i % 2 == 0 or ctx_out:
            mod_c = (jax.nn.silu(c_ctx[None, :]) @ mod_w[i] + mod_b[i])[:, None, :]
            csh1, csc1, cg1, csh2, csc2, cg2 = jnp.split(mod_c, 6, axis=-1)
            h_ctx = modulate(ctx, csh1, csc1)
        if i % 2 == 0:
            y, y_ctx = mixer_ab(h, h_ctx, ab_w_in[j], ab_w_out[j], na_rpb[j], gm_w_s[j], gm_b_s[j], ctx_out)
        else:
            hy_args = (hy_w_in[j], hy_b_in[j], hy_conv_w[j], hy_conv_b[j], hy_ffn_w1[j], hy_ffn_b1[j],
                       hy_ffn_f1[j], hy_ffn_w2[j], hy_ffn_b2[j], hy_ffn_f2[j], hy_ffn_w3[j], hy_d[j], hy_w_out[j])
            y = hyena(h, *hy_args)
            y_ctx = hyena(h_ctx, *hy_args) if ctx_out else None
        x = layer_norm(DN_ALPHA * x + g1 * y, ln_g[i, 0], ln_b[i, 0])
        x = layer_norm(DN_ALPHA * x + g2 * peer(modulate(x, sh2, sc2), peer_wq[i], peer_keys[i], peer_u[i], peer_v[i]),
                       ln_g[i, 1], ln_b[i, 1])
        if ctx_out:
            ctx = layer_norm(DN_ALPHA * ctx + cg1 * y_ctx, ln_g[i, 0], ln_b[i, 0])
            ctx = layer_norm(DN_ALPHA * ctx + cg2 * peer(modulate(ctx, csh2, csc2), peer_wq[i], peer_keys[i], peer_u[i], peer_v[i]),
                             ln_g[i, 1], ln_b[i, 1])
    return x
```

```python
import functools
import math

import jax
import jax.numpy as jnp
from jax import lax
from jax.experimental import pallas as pl
from jax.experimental.pallas import tpu as pltpu

D_MODEL = 1024
DEPTH = 2
GRID_W = 64
NA_HEADS = 8
NA_HEAD_DIM = 64
NA_DIM = NA_HEADS * NA_HEAD_DIM
NA_WIN_ROWS = 8
NA_WIN_COLS = 16
GM_GROUPS = 4
GM_GROUP_DIM = 128
GM_WIDTH = GM_GROUPS * GM_GROUP_DIM
GM_CHUNK = 128
AB_QKV = 3 * NA_DIM
HY_WIDTH = D_MODEL
HY_BANDS = 16
HY_DECAY_TARGET = 1e-2
HY_FAST_PCT = 0.3
HY_SLOW_PCT = 1.5
PEER_HEADS = 8
PEER_NKEYS = 128
PEER_DK = 256
PEER_TOPK = 16
PEER_BLOCK = 128
DN_ALPHA = (2 * DEPTH) ** 0.25
LN_EPS = 1e-5
F32 = jnp.float32
BF16 = jnp.bfloat16

VMEM_LIMIT = 56 * 1024 * 1024


def _mod_mm_kernel(x_ref, sc_ref, sh_ref, w_ref, o_ref):
    h = x_ref[0] * (1.0 + sc_ref[0]) + sh_ref[0]
    o_ref[0] = jnp.dot(h.astype(BF16), w_ref[...], preferred_element_type=F32)


def mod_matmul(x, scale, shift, w, tm=512):
    B, L, K = x.shape
    N = w.shape[1]
    return pl.pallas_call(
        _mod_mm_kernel,
        grid=(B, L // tm),
        in_specs=[
            pl.BlockSpec((1, tm, K), lambda b, i: (b, i, 0)),
            pl.BlockSpec((1, 1, K), lambda b, i: (b, 0, 0)),
            pl.BlockSpec((1, 1, K), lambda b, i: (b, 0, 0)),
            pl.BlockSpec((K, N), lambda b, i: (0, 0)),
        ],
        out_specs=pl.BlockSpec((1, tm, N), lambda b, i: (b, i, 0)),
        out_shape=jax.ShapeDtypeStruct((B, L, N), F32),
        compiler_params=pltpu.CompilerParams(
            dimension_semantics=("arbitrary", "arbitrary"), vmem_limit_bytes=VMEM_LIMIT),
        name="mod_matmul",
    )(x, scale, shift, w.astype(BF16))


def matmul(x, w, tm=512):
    B, L, K = x.shape
    zeros = jnp.zeros((B, 1, K), F32)
    return mod_matmul(x, zeros, zeros, w, tm=tm)


def normalize(x):
    mu = jnp.mean(x, axis=-1, keepdims=True)
    var = jnp.mean(jnp.square(x - mu), axis=-1, keepdims=True)
    return (x - mu) * lax.rsqrt(var + LN_EPS)


def layer_norm(x, g, b):
    return normalize(x) * g + b


def split_heads(t):
    return t.reshape(t.shape[0], t.shape[1], NA_HEADS, NA_HEAD_DIM)


def neighbourhood_attention(q, k, v, k_ctx, v_ctx, rpb):
    B, L, H, dh = q.shape
    rows = L // GRID_W
    wr = min(NA_WIN_ROWS, rows)
    wc = NA_WIN_COLS
    qg = q.reshape(B, rows, GRID_W, H, dh)
    kg = k.reshape(B, rows, GRID_W, H, dh)
    vg = v.reshape(B, rows, GRID_W, H, dh)
    cols = jnp.arange(GRID_W)
    col_start = jnp.clip(cols - wc // 2, 0, GRID_W - wc)
    col_idx = col_start[:, None] + jnp.arange(wc)[None, :]
    col_rel = col_idx - cols[:, None] + (NA_WIN_COLS - 1)
    scale = dh ** -0.5

    def row_block(r):
        rs = jnp.clip(r - wr // 2, 0, rows - wr)
        q_r = lax.dynamic_index_in_dim(qg, r, axis=1, keepdims=False)
        k_r = lax.dynamic_slice_in_dim(kg, rs, wr, axis=1)[:, :, col_idx]
        v_r = lax.dynamic_slice_in_dim(vg, rs, wr, axis=1)[:, :, col_idx]
        row_rel = rs + jnp.arange(wr) - r + (NA_WIN_ROWS - 1)
        bias = rpb[:, row_rel][:, :, col_rel].transpose(0, 2, 1, 3)
        s_loc = jnp.einsum('bqhd,brqwhd->bhqrw', q_r, k_r).astype(F32) * scale + bias[None].astype(F32)
        s_ctx = jnp.einsum('bqhd,bkhd->bhqk', q_r, k_ctx).astype(F32) * scale
        logits = jnp.concatenate([s_loc.reshape(B, H, GRID_W, wr * wc), s_ctx], axis=-1)
        p = jax.nn.softmax(logits, axis=-1).astype(v.dtype)
        p_loc = p[..., :wr * wc].reshape(B, H, GRID_W, wr, wc)
        p_ctx = p[..., wr * wc:]
        return (jnp.einsum('bhqrw,brqwhd->bqhd', p_loc, v_r)
                + jnp.einsum('bhqk,bkhd->bqhd', p_ctx, v_ctx))

    out = lax.map(row_block, jnp.arange(rows))
    return out.transpose(1, 0, 2, 3, 4).reshape(B, L, H * dh)


def chunk_spatial_gating(u, gv, w_s, b_s):
    B, L, _ = u.shape
    n = L // GM_CHUNK
    gvc = normalize(gv).reshape(B, n, GM_CHUNK, GM_GROUPS, GM_GROUP_DIM)
    mixed = jnp.einsum('gpq,bnqgc->bnpgc', w_s, gvc) + b_s.T[None, None, :, :, None]
    return u * mixed.reshape(B, L, GM_WIDTH)


def hyena_filter(L, w1, b1, f1, w2, b2, f2, w3):
    t = jnp.linspace(0.0, 1.0, L, dtype=F32)[:, None]
    ang = 2.0 * math.pi * jnp.arange(L, dtype=F32)[:, None] / L
    bands = jnp.linspace(1e-4, HY_BANDS - 1, HY_BANDS, dtype=F32)[None, :]
    z = jnp.concatenate([t, jnp.cos(bands * ang), jnp.sin(-bands * ang)], axis=-1)
    hid = jnp.sin(f1 * (z @ w1 + b1))
    hid = jnp.sin(f2 * (hid @ w2 + b2))
    filt = hid @ w3
    deltas = jnp.abs(jnp.linspace(math.log(HY_DECAY_TARGET) / HY_SLOW_PCT,
                                  math.log(HY_DECAY_TARGET) / HY_FAST_PCT, HY_WIDTH, dtype=F32))
    decay = jnp.exp(-t * deltas[None, :])
    fwd = filt[:, :HY_WIDTH] * decay
    bwd = filt[:, HY_WIDTH:] * decay
    l1 = jnp.sum(jnp.abs(fwd), axis=0) + jnp.sum(jnp.abs(bwd[1:]), axis=0)
    k2 = jnp.concatenate([fwd, jnp.zeros((1, HY_WIDTH), F32), bwd[:0:-1]], axis=0)
    return k2 / l1[None, :]


def long_conv(v, k2):
    L = v.shape[1]
    vf = jnp.fft.rfft(v, n=2 * L, axis=1)
    kf = jnp.fft.rfft(k2, n=2 * L, axis=0)
    return jnp.fft.irfft(vf * kf[None], n=2 * L, axis=1)[:, :L]


def hyena(x, sc, sh, w_in, b_in, conv_w, conv_b, w1, b1, f1, w2, b2, f2, w3, d_skip, w_out):
    L = x.shape[1]
    z = mod_matmul(x, sc, sh, w_in) + b_in
    z = lax.conv_general_dilated(z, conv_w[:, None, :], window_strides=(1,), padding=[(1, 1)],
                                 dimension_numbers=('NWC', 'WIO', 'NWC'),
                                 feature_group_count=3 * HY_WIDTH) + conv_b
    x0, x1, v = jnp.split(z, 3, axis=-1)
    k2 = hyena_filter(L, w1, b1, f1, w2, b2, f2, w3)
    v = v * x1
    y = (long_conv(v, k2) + v * d_skip) * x0
    return matmul(y, w_out)


def peer(x, sc, sh, wq, keys, u_tab, v_tab):
    B, L, D = x.shape
    n_tok = B * L
    xt = (x * (1 + sc) + sh).reshape(n_tok, D)
    q = mod_matmul(x, sc, sh, wq).reshape(n_tok, PEER_HEADS, 2, PEER_DK // 2)
    s = jnp.einsum('nhpd,hpkd->nhpk', q, keys).astype(F32)
    s1, i1 = lax.top_k(s[:, :, 0], PEER_TOPK)
    s2, i2 = lax.top_k(s[:, :, 1], PEER_TOPK)
    cand_s = (s1[..., :, None] + s2[..., None, :]).reshape(n_tok, PEER_HEADS, PEER_TOPK * PEER_TOPK)
    cand_i = (i1[..., :, None] * PEER_NKEYS + i2[..., None, :]).reshape(n_tok, PEER_HEADS, PEER_TOPK * PEER_TOPK)
    top_s, pos = lax.top_k(cand_s, PEER_TOPK)
    idx = jnp.take_along_axis(cand_i, pos, axis=-1)
    gate = jax.nn.softmax(top_s, axis=-1)
    nblk = n_tok // PEER_BLOCK

    def block(args):
        xb, ib, gb = args
        act = jax.nn.gelu(jnp.einsum('nd,nhkd->nhk', xb, u_tab[ib]).astype(F32), approximate=False)
        w = gb * act
        return jnp.einsum('nhk,nhkd->nd', w, v_tab[ib])

    out = lax.map(block, (xt.reshape(nblk, PEER_BLOCK, D),
                          idx.reshape(nblk, PEER_BLOCK, PEER_HEADS, PEER_TOPK),
                          gate.reshape(nblk, PEER_BLOCK, PEER_HEADS, PEER_TOPK)))
    return out.reshape(B, L, D)


def kernel(x, c, ctx, c_ctx, mod_w, mod_b, ln_g, ln_b, ab_w_in, ab_w_out, na_rpb, gm_w_s, gm_b_s, hy_w_in, hy_b_in, hy_conv_w, hy_conv_b, hy_ffn_w1, hy_ffn_b1, hy_ffn_f1, hy_ffn_w2, hy_ffn_b2, hy_ffn_f2, hy_ffn_w3, hy_d, hy_w_out, peer_wq, peer_keys, peer_u, peer_v):
    for i in range(DEPTH):
        j = i // 2
        mod = (jax.nn.silu(c) @ mod_w[i] + mod_b[i])[:, None, :]
        sh1, sc1, g1, sh2, sc2, g2 = jnp.split(mod, 6, axis=-1)
        if i % 2 == 0:
            mod_c = (jax.nn.silu(c_ctx[None, :]) @ mod_w[i] + mod_b[i])[:, None, :]
            csh1, csc1 = mod_c[..., :D_MODEL], mod_c[..., D_MODEL:2 * D_MODEL]
            p = mod_matmul(x, sc1, sh1, ab_w_in[j])
            q, k, v = [split_heads(t) for t in jnp.split(p[..., :AB_QKV], 3, axis=-1)]
            u, gv = jnp.split(jax.nn.gelu(p[..., AB_QKV:], approximate=False), 2, axis=-1)
            B = x.shape[0]
            pc = mod_matmul(ctx, jnp.broadcast_to(csc1, (B, 1, D_MODEL)), jnp.broadcast_to(csh1, (B, 1, D_MODEL)),
                            ab_w_in[j][:, NA_DIM:AB_QKV], tm=256)
            kc, vc = [split_heads(t) for t in jnp.split(pc, 2, axis=-1)]
            a = neighbourhood_attention(q, k, v, kc, vc, na_rpb[j])
            g = chunk_spatial_gating(u, gv, gm_w_s[j], gm_b_s[j])
            y = matmul(jnp.concatenate([a, g], axis=-1), ab_w_out[j])
        else:
            y = hyena(x, sc1, sh1, hy_w_in[j], hy_b_in[j], hy_conv_w[j], hy_conv_b[j], hy_ffn_w1[j], hy_ffn_b1[j],
                      hy_ffn_f1[j], hy_ffn_w2[j], hy_ffn_b2[j], hy_ffn_f2[j], hy_ffn_w3[j], hy_d[j], hy_w_out[j])
        x = layer_norm(DN_ALPHA * x + g1 * y, ln_g[i, 0], ln_b[i, 0])
        x = layer_norm(DN_ALPHA * x + g2 * peer(x, sc2, sh2, peer_wq[i], peer_keys[i], peer_u[i], peer_v[i]),
                       ln_g[i, 1], ln_b[i, 1])
    return x
```

```python
import math

import jax
import jax.numpy as jnp
from jax import lax
from jax.experimental import pallas as pl
from jax.experimental.pallas import tpu as pltpu

D_MODEL = 1024
DEPTH = 2
GRID_W = 64
NA_HEADS = 8
NA_HEAD_DIM = 64
NA_DIM = NA_HEADS * NA_HEAD_DIM
NA_WIN_ROWS = 8
NA_WIN_COLS = 16
GM_GROUPS = 4
GM_GROUP_DIM = 128
GM_WIDTH = GM_GROUPS * GM_GROUP_DIM
GM_CHUNK = 128
AB_QKV = 3 * NA_DIM
HY_WIDTH = D_MODEL
HY_BANDS = 16
HY_DECAY_TARGET = 1e-2
HY_FAST_PCT = 0.3
HY_SLOW_PCT = 1.5
PEER_HEADS = 8
PEER_NKEYS = 128
PEER_DK = 256
PEER_TOPK = 16
DN_ALPHA = (2 * DEPTH) ** 0.25
LN_EPS = 1e-5
F32 = jnp.float32
BF16 = jnp.bfloat16

VMEM_LIMIT = 56 * 1024 * 1024


def _mod_mm_kernel(x_ref, sc_ref, sh_ref, w_ref, o_ref):
    h = x_ref[0] * (1.0 + sc_ref[0]) + sh_ref[0]
    o_ref[0] = jnp.dot(h.astype(BF16), w_ref[...], preferred_element_type=F32)


def mod_matmul(x, scale, shift, w, tm=512):
    B, L, K = x.shape
    N = w.shape[1]
    return pl.pallas_call(
        _mod_mm_kernel,
        grid=(B, L // tm),
        in_specs=[
            pl.BlockSpec((1, tm, K), lambda b, i: (b, i, 0)),
            pl.BlockSpec((1, 1, K), lambda b, i: (b, 0, 0)),
            pl.BlockSpec((1, 1, K), lambda b, i: (b, 0, 0)),
            pl.BlockSpec((K, N), lambda b, i: (0, 0)),
        ],
        out_specs=pl.BlockSpec((1, tm, N), lambda b, i: (b, i, 0)),
        out_shape=jax.ShapeDtypeStruct((B, L, N), F32),
        compiler_params=pltpu.CompilerParams(
            dimension_semantics=("arbitrary", "arbitrary"), vmem_limit_bytes=VMEM_LIMIT),
        name="mod_matmul",
    )(x, scale, shift, w.astype(BF16))


def matmul(x, w, tm=512):
    B, L, K = x.shape
    zeros = jnp.zeros((B, 1, K), F32)
    return mod_matmul(x, zeros, zeros, w, tm=tm)


def normalize(x):
    mu = jnp.mean(x, axis=-1, keepdims=True)
    var = jnp.mean(jnp.square(x - mu), axis=-1, keepdims=True)
    return (x - mu) * lax.rsqrt(var + LN_EPS)


def layer_norm(x, g, b):
    return normalize(x) * g + b


NA_RB = 8
NA_NEG = -1e30


def _na_kernel(q_ref, ka_ref, kb_ref, kc_ref, va_ref, vb_ref, vc_ref, kx_ref, vx_ref, bias_ref, o_ref,
               kbuf, vbuf):
    j = pl.program_id(1)
    W, H, dh, WR = GRID_W, NA_HEADS, NA_HEAD_DIM, NA_WIN_ROWS
    nb = W * NA_RB
    rows = pl.num_programs(1) * NA_RB
    for s, (kr, vr) in enumerate(((ka_ref, va_ref), (kb_ref, vb_ref), (kc_ref, vc_ref))):
        kbuf[s * nb:(s + 1) * nb, :] = kr[0].astype(BF16)
        vbuf[s * nb:(s + 1) * nb, :] = vr[0].astype(BF16)
    kx = kx_ref[0].astype(BF16)
    vx = vx_ref[0].astype(BF16)
    row_head = lax.broadcasted_iota(jnp.int32, (H * W, H * dh), 0) // W
    col_head = lax.broadcasted_iota(jnp.int32, (H * W, H * dh), 1) // dh
    head_mask = row_head == col_head
    nt = (((1,), (1,)), ((), ()))

    def row_body(rl, carry):
        r = j * NA_RB + rl
        rs = jnp.clip(r - WR // 2, 0, rows - WR)
        off = pl.multiple_of((rs - (j - 1) * NA_RB) * W, W)
        d0 = rs - r + (WR - 1)
        q_r = q_ref[0, pl.ds(pl.multiple_of(rl * W, W), W), :] * (dh ** -0.5)
        q_st = jnp.where(head_mask, jnp.concatenate([q_r] * H, axis=0), 0.0).astype(BF16)
        k_win = kbuf[pl.ds(off, WR * W), :]
        v_win = vbuf[pl.ds(off, WR * W), :]
        s_loc = lax.dot_general(q_st, k_win, nt, preferred_element_type=F32) + bias_ref[d0]
        s_ctx = lax.dot_general(q_st, kx, nt, preferred_element_type=F32)
        m = jnp.maximum(jnp.max(s_loc, axis=-1, keepdims=True), jnp.max(s_ctx, axis=-1, keepdims=True))
        p_loc = jnp.exp(s_loc - m)
        p_ctx = jnp.exp(s_ctx - m)
        den = jnp.sum(p_loc, axis=-1, keepdims=True) + jnp.sum(p_ctx, axis=-1, keepdims=True)
        o_all = (jnp.dot(p_loc.astype(BF16), v_win, preferred_element_type=F32)
                 + jnp.dot(p_ctx.astype(BF16), vx, preferred_element_type=F32)) / den
        o_all = jnp.where(head_mask, o_all, 0.0)
        out = o_all[0:W]
        for h in range(1, H):
            out = out + o_all[h * W:(h + 1) * W]
        o_ref[0, pl.ds(pl.multiple_of(rl * W, W), W), :] = out
        return carry

    lax.fori_loop(0, NA_RB, row_body, 0)


def _na_bias_table(rpb):
    W, H, WR, WC = GRID_W, NA_HEADS, NA_WIN_ROWS, NA_WIN_COLS
    qc = jnp.arange(W)
    cs = jnp.clip(qc - WC // 2, 0, W - WC)
    kc = jnp.arange(W)
    valid = (kc[None, :] >= cs[:, None]) & (kc[None, :] < cs[:, None] + WC)
    crel = jnp.clip(kc[None, :] - qc[:, None] + (WC - 1), 0, 2 * WC - 2)
    rr = jnp.arange(WR)[:, None] + jnp.arange(WR)[None, :]
    t = rpb[:, rr]
    t = t[:, :, :, crel]
    t = jnp.where(valid[None, None, None], t, NA_NEG)
    return t.transpose(1, 0, 3, 2, 4).reshape(WR, H * W, WR * W).astype(F32)


def neighbourhood_attention(p, pc, rpb):
    B, L, _ = p.shape
    Lc = pc.shape[1]
    nb = GRID_W * NA_RB
    nj = L // nb
    blk = (1, nb, NA_DIM)
    return pl.pallas_call(
        _na_kernel,
        grid=(B, nj),
        in_specs=[
            pl.BlockSpec(blk, lambda b, j: (b, j, 0)),
            pl.BlockSpec(blk, lambda b, j: (b, jnp.maximum(j - 1, 0), 1)),
            pl.BlockSpec(blk, lambda b, j: (b, j, 1)),
            pl.BlockSpec(blk, lambda b, j: (b, jnp.minimum(j + 1, nj - 1), 1)),
            pl.BlockSpec(blk, lambda b, j: (b, jnp.maximum(j - 1, 0), 2)),
            pl.BlockSpec(blk, lambda b, j: (b, j, 2)),
            pl.BlockSpec(blk, lambda b, j: (b, jnp.minimum(j + 1, nj - 1), 2)),
            pl.BlockSpec((1, Lc, NA_DIM), lambda b, j: (b, 0, 0)),
            pl.BlockSpec((1, Lc, NA_DIM), lambda b, j: (b, 0, 1)),
            pl.BlockSpec((NA_WIN_ROWS, NA_HEADS * GRID_W, NA_WIN_ROWS * GRID_W), lambda b, j: (0, 0, 0)),
        ],
        out_specs=pl.BlockSpec(blk, lambda b, j: (b, j, 0)),
        out_shape=jax.ShapeDtypeStruct((B, L, NA_DIM), F32),
        scratch_shapes=[pltpu.VMEM((3 * nb, NA_DIM), BF16), pltpu.VMEM((3 * nb, NA_DIM), BF16)],
        compiler_params=pltpu.CompilerParams(
            dimension_semantics=("arbitrary", "arbitrary"), vmem_limit_bytes=VMEM_LIMIT),
        name="na_attention",
    )(p, p, p, p, p, p, p, pc, pc, _na_bias_table(rpb))


PEER_TB = 8
PEER_NSEL = PEER_HEADS * PEER_TOPK
PEER_NBUF = 3


def _gelu_exact(x):
    return 0.5 * x * (1.0 + lax.erf(x * (2.0 ** -0.5)))


def _peer_kernel(idx0_ref, idx1_ref, idx2_ref, x_ref, sc_ref, sh_ref, g2_ref, gate_ref, lng_ref, lnb_ref, uv_hbm,
                 o_ref, buf, sem):
    i = pl.program_id(0)
    n = pl.num_programs(0)
    D = x_ref.shape[-1]
    slot = i % PEER_NBUF
    slot_ahead = (i + PEER_NBUF - 1) % PEER_NBUF
    half = PEER_NSEL // 2

    def issue(ids_ref, dst_slot, t, k0, k1):
        for k in range(k0, k1):
            e = ids_ref[0, 0, t * PEER_NSEL + k]
            pltpu.make_async_copy(uv_hbm.at[e], buf.at[dst_slot, t, pl.ds(k, 1)],
                                  sem.at[dst_slot]).start(priority=k % 2)

    def wait_slot(s):
        pltpu.make_async_copy(buf.at[s], buf.at[s], sem.at[s]).wait()

    @pl.when(i == 0)
    def _():
        def body(t, carry):
            issue(idx0_ref, 0, t, 0, PEER_NSEL)
            issue(idx1_ref, 1, t, 0, PEER_NSEL)
            return carry
        lax.fori_loop(0, PEER_TB, body, 0)

    wait_slot(slot)

    x = x_ref[...]
    xm = x * (1.0 + sc_ref[0]) + sh_ref[0]
    acts = []
    for t in range(PEER_TB):
        issue(idx2_ref, slot_ahead, t, 0, half)
        u = buf[slot, t, :, 0:D]
        acts.append(jnp.sum(u * xm[t:t + 1, :], axis=-1, keepdims=True))
    w = gate_ref[0] * _gelu_exact(jnp.concatenate(acts, axis=1))
    outs = []
    for t in range(PEER_TB):
        issue(idx2_ref, slot_ahead, t, half, PEER_NSEL)
        v = buf[slot, t, :, D:2 * D]
        outs.append(jnp.sum(w[:, t:t + 1] * v, axis=0, keepdims=True))

    @pl.when(i == n - 1)
    def _():
        wait_slot((i + 1) % PEER_NBUF)
        wait_slot(slot_ahead)

    y = DN_ALPHA * x + g2_ref[0] * jnp.concatenate(outs, axis=0)
    mu = jnp.mean(y, axis=-1, keepdims=True)
    yc = y - mu
    var = jnp.mean(yc * yc, axis=-1, keepdims=True)
    o_ref[...] = yc * lax.rsqrt(var + LN_EPS) * lng_ref[...] + lnb_ref[...]


def peer_apply(x, sc, sh, g2, lng, lnb, idx, gate, u_tab, v_tab):
    B, L, D = x.shape
    N = B * L
    nblk = N // PEER_TB
    nrow = PEER_TB * PEER_NSEL
    uv = jnp.concatenate([u_tab, v_tab], axis=1)[:, None, :]
    idx3 = idx.reshape(nblk, 1, nrow)
    gate_t = gate.reshape(nblk, PEER_TB, PEER_NSEL).transpose(0, 2, 1)
    per_b = L // PEER_TB
    vec = lambda i: (i // per_b, 0, 0)
    out = pl.pallas_call(
        _peer_kernel,
        grid=(nblk,),
        in_specs=[
            pl.BlockSpec((1, 1, nrow), lambda i: (i, 0, 0), memory_space=pltpu.SMEM),
            pl.BlockSpec((1, 1, nrow), lambda i: (jnp.minimum(i + 1, nblk - 1), 0, 0), memory_space=pltpu.SMEM),
            pl.BlockSpec((1, 1, nrow), lambda i: (jnp.minimum(i + 2, nblk - 1), 0, 0), memory_space=pltpu.SMEM),
            pl.BlockSpec((PEER_TB, D), lambda i: (i, 0)),
            pl.BlockSpec((1, 1, D), vec),
            pl.BlockSpec((1, 1, D), vec),
            pl.BlockSpec((1, 1, D), vec),
            pl.BlockSpec((1, PEER_NSEL, PEER_TB), lambda i: (i, 0, 0)),
            pl.BlockSpec((1, D), lambda i: (0, 0)),
            pl.BlockSpec((1, D), lambda i: (0, 0)),
            pl.BlockSpec(memory_space=pl.ANY),
        ],
        out_specs=pl.BlockSpec((PEER_TB, D), lambda i: (i, 0)),
        out_shape=jax.ShapeDtypeStruct((N, D), F32),
        scratch_shapes=[pltpu.VMEM((PEER_NBUF, PEER_TB, PEER_NSEL, 2 * D), F32),
                        pltpu.SemaphoreType.DMA((PEER_NBUF,))],
        compiler_params=pltpu.CompilerParams(
            dimension_semantics=("arbitrary",), vmem_limit_bytes=VMEM_LIMIT, disable_bounds_checks=True),
        name="peer_gather",
    )(idx3, idx3, idx3, x.reshape(N, D), sc, sh, g2, gate_t, lng[None, :], lnb[None, :], uv)
    return out.reshape(B, L, D)


def chunk_spatial_gating(u, gv, w_s, b_s):
    B, L, _ = u.shape
    n = L // GM_CHUNK
    gvc = normalize(gv).reshape(B, n, GM_CHUNK, GM_GROUPS, GM_GROUP_DIM)
    mixed = jnp.einsum('gpq,bnqgc->bnpgc', w_s, gvc) + b_s.T[None, None, :, :, None]
    return u * mixed.reshape(B, L, GM_WIDTH)


def hyena_filter(L, w1, b1, f1, w2, b2, f2, w3):
    t = jnp.linspace(0.0, 1.0, L, dtype=F32)[:, None]
    ang = 2.0 * math.pi * jnp.arange(L, dtype=F32)[:, None] / L
    bands = jnp.linspace(1e-4, HY_BANDS - 1, HY_BANDS, dtype=F32)[None, :]
    z = jnp.concatenate([t, jnp.cos(bands * ang), jnp.sin(-bands * ang)], axis=-1)
    hid = jnp.sin(f1 * (z @ w1 + b1))
    hid = jnp.sin(f2 * (hid @ w2 + b2))
    filt = hid @ w3
    deltas = jnp.abs(jnp.linspace(math.log(HY_DECAY_TARGET) / HY_SLOW_PCT,
                                  math.log(HY_DECAY_TARGET) / HY_FAST_PCT, HY_WIDTH, dtype=F32))
    decay = jnp.exp(-t * deltas[None, :])
    fwd = filt[:, :HY_WIDTH] * decay
    bwd = filt[:, HY_WIDTH:] * decay
    l1 = jnp.sum(jnp.abs(fwd), axis=0) + jnp.sum(jnp.abs(bwd[1:]), axis=0)
    k2 = jnp.concatenate([fwd, jnp.zeros((1, HY_WIDTH), F32), bwd[:0:-1]], axis=0)
    return k2 / l1[None, :]


def long_conv(v, k2):
    L = v.shape[1]
    vf = jnp.fft.rfft(v, n=2 * L, axis=1)
    kf = jnp.fft.rfft(k2, n=2 * L, axis=0)
    return jnp.fft.irfft(vf * kf[None], n=2 * L, axis=1)[:, :L]


def hyena(x, sc, sh, w_in, b_in, conv_w, conv_b, w1, b1, f1, w2, b2, f2, w3, d_skip, w_out):
    L = x.shape[1]
    z = mod_matmul(x, sc, sh, w_in) + b_in
    z = lax.conv_general_dilated(z, conv_w[:, None, :], window_strides=(1,), padding=[(1, 1)],
                                 dimension_numbers=('NWC', 'WIO', 'NWC'),
                                 feature_group_count=3 * HY_WIDTH) + conv_b
    x0, x1, v = jnp.split(z, 3, axis=-1)
    k2 = hyena_filter(L, w1, b1, f1, w2, b2, f2, w3)
    v = v * x1
    y = (long_conv(v, k2) + v * d_skip) * x0
    return matmul(y, w_out)


def peer_select(x, sc, sh, wq, keys):
    B, L, D = x.shape
    n_tok = B * L
    q = mod_matmul(x, sc, sh, wq).reshape(n_tok, PEER_HEADS, 2, PEER_DK // 2)
    s = jnp.einsum('nhpd,hpkd->nhpk', q, keys).astype(F32)
    s1, i1 = lax.top_k(s[:, :, 0], PEER_TOPK)
    s2, i2 = lax.top_k(s[:, :, 1], PEER_TOPK)
    cand_s = (s1[..., :, None] + s2[..., None, :]).reshape(n_tok, PEER_HEADS, PEER_TOPK * PEER_TOPK)
    cand_i = (i1[..., :, None] * PEER_NKEYS + i2[..., None, :]).reshape(n_tok, PEER_HEADS, PEER_TOPK * PEER_TOPK)
    top_s, pos = lax.top_k(cand_s, PEER_TOPK)
    idx = jnp.take_along_axis(cand_i, pos, axis=-1)
    gate = jax.nn.softmax(top_s, axis=-1)
    return idx.reshape(n_tok, PEER_NSEL), gate.reshape(n_tok, PEER_NSEL)


def kernel(x, c, ctx, c_ctx, mod_w, mod_b, ln_g, ln_b, ab_w_in, ab_w_out, na_rpb, gm_w_s, gm_b_s, hy_w_in, hy_b_in, hy_conv_w, hy_conv_b, hy_ffn_w1, hy_ffn_b1, hy_ffn_f1, hy_ffn_w2, hy_ffn_b2, hy_ffn_f2, hy_ffn_w3, hy_d, hy_w_out, peer_wq, peer_keys, peer_u, peer_v):
    for i in range(DEPTH):
        j = i // 2
        mod = (jax.nn.silu(c) @ mod_w[i] + mod_b[i])[:, None, :]
        sh1, sc1, g1, sh2, sc2, g2 = jnp.split(mod, 6, axis=-1)
        if i % 2 == 0:
            mod_c = (jax.nn.silu(c_ctx[None, :]) @ mod_w[i] + mod_b[i])[:, None, :]
            csh1, csc1 = mod_c[..., :D_MODEL], mod_c[..., D_MODEL:2 * D_MODEL]
            p = mod_matmul(x, sc1, sh1, ab_w_in[j])
            u, gv = jnp.split(jax.nn.gelu(p[..., AB_QKV:], approximate=False), 2, axis=-1)
            B = x.shape[0]
            pc = mod_matmul(ctx, jnp.broadcast_to(csc1, (B, 1, D_MODEL)), jnp.broadcast_to(csh1, (B, 1, D_MODEL)),
                            ab_w_in[j][:, NA_DIM:AB_QKV], tm=256)
            a = neighbourhood_attention(p, pc, na_rpb[j])
            g = chunk_spatial_gating(u, gv, gm_w_s[j], gm_b_s[j])
            y = matmul(jnp.concatenate([a, g], axis=-1), ab_w_out[j])
        else:
            y = hyena(x, sc1, sh1, hy_w_in[j], hy_b_in[j], hy_conv_w[j], hy_conv_b[j], hy_ffn_w1[j], hy_ffn_b1[j],
                      hy_ffn_f1[j], hy_ffn_w2[j], hy_ffn_b2[j], hy_ffn_f2[j], hy_ffn_w3[j], hy_d[j], hy_w_out[j])
        x = layer_norm(DN_ALPHA * x + g1 * y, ln_g[i, 0], ln_b[i, 0])
        idx, gate = peer_select(x, sc2, sh2, peer_wq[i], peer_keys[i])
        x = peer_apply(x, sc2, sh2, g2, ln_g[i, 1], ln_b[i, 1], idx, gate, peer_u[i], peer_v[i])
    return x
```

```python
import math

import jax
import jax.numpy as jnp
from jax import lax
from jax.experimental import pallas as pl
from jax.experimental.pallas import tpu as pltpu

D_MODEL = 1024
DEPTH = 2
GRID_W = 64
NA_HEADS = 8
NA_HEAD_DIM = 64
NA_DIM = NA_HEADS * NA_HEAD_DIM
NA_WIN_ROWS = 8
NA_WIN_COLS = 16
GM_GROUPS = 4
GM_GROUP_DIM = 128
GM_WIDTH = GM_GROUPS * GM_GROUP_DIM
GM_CHUNK = 128
AB_QKV = 3 * NA_DIM
HY_WIDTH = D_MODEL
HY_BANDS = 16
HY_DECAY_TARGET = 1e-2
HY_FAST_PCT = 0.3
HY_SLOW_PCT = 1.5
PEER_HEADS = 8
PEER_NKEYS = 128
PEER_DK = 256
PEER_TOPK = 16
DN_ALPHA = (2 * DEPTH) ** 0.25
LN_EPS = 1e-5
F32 = jnp.float32
BF16 = jnp.bfloat16

VMEM_LIMIT = 56 * 1024 * 1024


def _mod_mm_kernel(x_ref, sc_ref, sh_ref, w_ref, o_ref):
    h = x_ref[0] * (1.0 + sc_ref[0]) + sh_ref[0]
    o_ref[0] = jnp.dot(h.astype(BF16), w_ref[...], preferred_element_type=F32)


def mod_matmul(x, scale, shift, w, tm=512):
    B, L, K = x.shape
    N = w.shape[1]
    return pl.pallas_call(
        _mod_mm_kernel,
        grid=(B, L // tm),
        in_specs=[
            pl.BlockSpec((1, tm, K), lambda b, i: (b, i, 0)),
            pl.BlockSpec((1, 1, K), lambda b, i: (b, 0, 0)),
            pl.BlockSpec((1, 1, K), lambda b, i: (b, 0, 0)),
            pl.BlockSpec((K, N), lambda b, i: (0, 0)),
        ],
        out_specs=pl.BlockSpec((1, tm, N), lambda b, i: (b, i, 0)),
        out_shape=jax.ShapeDtypeStruct((B, L, N), F32),
        compiler_params=pltpu.CompilerParams(
            dimension_semantics=("arbitrary", "arbitrary"), vmem_limit_bytes=VMEM_LIMIT),
        name="mod_matmul",
    )(x, scale, shift, w.astype(BF16))


def matmul(x, w, tm=512):
    B, L, K = x.shape
    zeros = jnp.zeros((B, 1, K), F32)
    return mod_matmul(x, zeros, zeros, w, tm=tm)


def normalize(x):
    mu = jnp.mean(x, axis=-1, keepdims=True)
    var = jnp.mean(jnp.square(x - mu), axis=-1, keepdims=True)
    return (x - mu) * lax.rsqrt(var + LN_EPS)


def layer_norm(x, g, b):
    return normalize(x) * g + b


NA_RB = 8
NA_NEG = -1e30


def _na_kernel(q_ref, ka_ref, kb_ref, kc_ref, va_ref, vb_ref, vc_ref, kx_ref, vx_ref, bias_ref, o_ref,
               kbuf, vbuf):
    j = pl.program_id(1)
    W, H, dh, WR = GRID_W, NA_HEADS, NA_HEAD_DIM, NA_WIN_ROWS
    nb = W * NA_RB
    rows = pl.num_programs(1) * NA_RB
    for s, (kr, vr) in enumerate(((ka_ref, va_ref), (kb_ref, vb_ref), (kc_ref, vc_ref))):
        kbuf[s * nb:(s + 1) * nb, :] = kr[0].astype(BF16)
        vbuf[s * nb:(s + 1) * nb, :] = vr[0].astype(BF16)
    kx = kx_ref[0].astype(BF16)
    vx = vx_ref[0].astype(BF16)
    row_head = lax.broadcasted_iota(jnp.int32, (H * W, H * dh), 0) // W
    col_head = lax.broadcasted_iota(jnp.int32, (H * W, H * dh), 1) // dh
    head_mask = row_head == col_head
    nt = (((1,), (1,)), ((), ()))

    def row_body(rl, carry):
        r = j * NA_RB + rl
        rs = jnp.clip(r - WR // 2, 0, rows - WR)
        off = pl.multiple_of((rs - (j - 1) * NA_RB) * W, W)
        d0 = rs - r + (WR - 1)
        q_r = q_ref[0, pl.ds(pl.multiple_of(rl * W, W), W), :] * (dh ** -0.5)
        q_st = jnp.where(head_mask, jnp.concatenate([q_r] * H, axis=0), 0.0).astype(BF16)
        k_win = kbuf[pl.ds(off, WR * W), :]
        v_win = vbuf[pl.ds(off, WR * W), :]
        s_loc = lax.dot_general(q_st, k_win, nt, preferred_element_type=F32) + bias_ref[d0]
        s_ctx = lax.dot_general(q_st, kx, nt, preferred_element_type=F32)
        m = jnp.maximum(jnp.max(s_loc, axis=-1, keepdims=True), jnp.max(s_ctx, axis=-1, keepdims=True))
        p_loc = jnp.exp(s_loc - m)
        p_ctx = jnp.exp(s_ctx - m)
        den = jnp.sum(p_loc, axis=-1, keepdims=True) + jnp.sum(p_ctx, axis=-1, keepdims=True)
        o_all = (jnp.dot(p_loc.astype(BF16), v_win, preferred_element_type=F32)
                 + jnp.dot(p_ctx.astype(BF16), vx, preferred_element_type=F32)) / den
        o_all = jnp.where(head_mask, o_all, 0.0)
        out = o_all[0:W]
        for h in range(1, H):
            out = out + o_all[h * W:(h + 1) * W]
        o_ref[0, pl.ds(pl.multiple_of(rl * W, W), W), :] = out
        return carry

    lax.fori_loop(0, NA_RB, row_body, 0)


def _na_bias_table(rpb):
    W, H, WR, WC = GRID_W, NA_HEADS, NA_WIN_ROWS, NA_WIN_COLS
    qc = jnp.arange(W)
    cs = jnp.clip(qc - WC // 2, 0, W - WC)
    kc = jnp.arange(W)
    valid = (kc[None, :] >= cs[:, None]) & (kc[None, :] < cs[:, None] + WC)
    crel = jnp.clip(kc[None, :] - qc[:, None] + (WC - 1), 0, 2 * WC - 2)
    rr = jnp.arange(WR)[:, None] + jnp.arange(WR)[None, :]
    t = rpb[:, rr]
    t = t[:, :, :, crel]
    t = jnp.where(valid[None, None, None], t, NA_NEG)
    return t.transpose(1, 0, 3, 2, 4).reshape(WR, H * W, WR * W).astype(F32)


def neighbourhood_attention(p, pc, rpb):
    B, L, _ = p.shape
    Lc = pc.shape[1]
    nb = GRID_W * NA_RB
    nj = L // nb
    blk = (1, nb, NA_DIM)
    return pl.pallas_call(
        _na_kernel,
        grid=(B, nj),
        in_specs=[
            pl.BlockSpec(blk, lambda b, j: (b, j, 0)),
            pl.BlockSpec(blk, lambda b, j: (b, jnp.maximum(j - 1, 0), 1)),
            pl.BlockSpec(blk, lambda b, j: (b, j, 1)),
            pl.BlockSpec(blk, lambda b, j: (b, jnp.minimum(j + 1, nj - 1), 1)),
            pl.BlockSpec(blk, lambda b, j: (b, jnp.maximum(j - 1, 0), 2)),
            pl.BlockSpec(blk, lambda b, j: (b, j, 2)),
            pl.BlockSpec(blk, lambda b, j: (b, jnp.minimum(j + 1, nj - 1), 2)),
            pl.BlockSpec((1, Lc, NA_DIM), lambda b, j: (b, 0, 0)),
            pl.BlockSpec((1, Lc, NA_DIM), lambda b, j: (b, 0, 1)),
            pl.BlockSpec((NA_WIN_ROWS, NA_HEADS * GRID_W, NA_WIN_ROWS * GRID_W), lambda b, j: (0, 0, 0)),
        ],
        out_specs=pl.BlockSpec(blk, lambda b, j: (b, j, 0)),
        out_shape=jax.ShapeDtypeStruct((B, L, NA_DIM), F32),
        scratch_shapes=[pltpu.VMEM((3 * nb, NA_DIM), BF16), pltpu.VMEM((3 * nb, NA_DIM), BF16)],
        compiler_params=pltpu.CompilerParams(
            dimension_semantics=("arbitrary", "arbitrary"), vmem_limit_bytes=VMEM_LIMIT),
        name="na_attention",
    )(p, p, p, p, p, p, p, pc, pc, _na_bias_table(rpb))


PEER_TB = 8
PEER_NSEL = PEER_HEADS * PEER_TOPK
PEER_NBUF = 4
PEER_TS = PEER_TB * PEER_NBUF


def _gelu_exact(x):
    return 0.5 * x * (1.0 + lax.erf(x * (2.0 ** -0.5)))


def _peer_kernel(idc_ref, idn_ref, x_ref, sc_ref, sh_ref, g2_ref, gate_ref, lng_ref, lnb_ref, uv_hbm,
                 o_ref, buf, sem):
    g = pl.program_id(0)
    n = pl.num_programs(0)
    D = x_ref.shape[-1]
    ahead = PEER_NBUF - 1
    half = PEER_NSEL // 2

    def issue(ids_ref, src_blk, dst_slot, t, k0, k1):
        for k in range(k0, k1):
            e = ids_ref[0, 0, (src_blk * PEER_TB + t) * PEER_NSEL + k]
            pltpu.make_async_copy(uv_hbm.at[e], buf.at[dst_slot, t, pl.ds(k, 1)],
                                  sem.at[dst_slot]).start(priority=k % 2)

    def wait_slot(s):
        pltpu.make_async_copy(buf.at[s], buf.at[s], sem.at[s]).wait()

    @pl.when(g == 0)
    def _():
        for s in range(ahead):
            for t in range(PEER_TB):
                issue(idc_ref, s, s, t, 0, PEER_NSEL)

    x = x_ref[...]
    xm = x * (1.0 + sc_ref[0]) + sh_ref[0]
    gate = gate_ref[0]
    outs = []
    for s in range(PEER_NBUF):
        wait_slot(s)
        nb = s + ahead
        ids_ref, src_blk = (idc_ref, nb) if nb < PEER_NBUF else (idn_ref, nb - PEER_NBUF)
        dst = nb % PEER_NBUF
        acts = []
        for t in range(PEER_TB):
            issue(ids_ref, src_blk, dst, t, 0, half)
            row = s * PEER_TB + t
            acts.append(jnp.sum(buf[s, t, :, 0:D] * xm[row:row + 1, :], axis=-1, keepdims=True))
        w = gate[:, s * PEER_TB:(s + 1) * PEER_TB] * _gelu_exact(jnp.concatenate(acts, axis=1))
        for t in range(PEER_TB):
            issue(ids_ref, src_blk, dst, t, half, PEER_NSEL)
            outs.append(jnp.sum(w[:, t:t + 1] * buf[s, t, :, D:2 * D], axis=0, keepdims=True))

    @pl.when(g == n - 1)
    def _():
        for s in range(ahead):
            wait_slot(s)

    y = DN_ALPHA * x + g2_ref[0] * jnp.concatenate(outs, axis=0)
    mu = jnp.mean(y, axis=-1, keepdims=True)
    yc = y - mu
    var = jnp.mean(yc * yc, axis=-1, keepdims=True)
    o_ref[...] = yc * lax.rsqrt(var + LN_EPS) * lng_ref[...] + lnb_ref[...]


def peer_apply(x, sc, sh, g2, lng, lnb, idx_t, gate_t, u_tab, v_tab):
    B, L, D = x.shape
    N = B * L
    nstep = N // PEER_TS
    nid = PEER_TS * PEER_NSEL
    uv = jnp.concatenate([u_tab, v_tab], axis=1)[:, None, :]
    ids = idx_t.T.reshape(nstep, 1, nid)
    gate_b = gate_t.reshape(PEER_NSEL, nstep, PEER_TS).transpose(1, 0, 2)
    per_b = L // PEER_TS
    vec = lambda i: (i // per_b, 0, 0)
    out = pl.pallas_call(
        _peer_kernel,
        grid=(nstep,),
        in_specs=[
            pl.BlockSpec((1, 1, nid), lambda i: (i, 0, 0), memory_space=pltpu.SMEM),
            pl.BlockSpec((1, 1, nid), lambda i: (jnp.minimum(i + 1, nstep - 1), 0, 0), memory_space=pltpu.SMEM),
            pl.BlockSpec((PEER_TS, D), lambda i: (i, 0)),
            pl.BlockSpec((1, 1, D), vec),
            pl.BlockSpec((1, 1, D), vec),
            pl.BlockSpec((1, 1, D), vec),
            pl.BlockSpec((1, PEER_NSEL, PEER_TS), lambda i: (i, 0, 0)),
            pl.BlockSpec((1, D), lambda i: (0, 0)),
            pl.BlockSpec((1, D), lambda i: (0, 0)),
            pl.BlockSpec(memory_space=pl.ANY),
        ],
        out_specs=pl.BlockSpec((PEER_TS, D), lambda i: (i, 0)),
        out_shape=jax.ShapeDtypeStruct((N, D), F32),
        scratch_shapes=[pltpu.VMEM((PEER_NBUF, PEER_TB, PEER_NSEL, 2 * D), F32),
                        pltpu.SemaphoreType.DMA((PEER_NBUF,))],
        compiler_params=pltpu.CompilerParams(
            dimension_semantics=("arbitrary",), vmem_limit_bytes=VMEM_LIMIT, disable_bounds_checks=True),
        name="peer_gather",
    )(ids, ids, x.reshape(N, D), sc, sh, g2, gate_b, lng[None, :], lnb[None, :], uv)
    return out.reshape(B, L, D)


def chunk_spatial_gating(u, gv, w_s, b_s):
    B, L, _ = u.shape
    n = L // GM_CHUNK
    gvc = normalize(gv).reshape(B, n, GM_CHUNK, GM_GROUPS, GM_GROUP_DIM)
    mixed = jnp.einsum('gpq,bnqgc->bnpgc', w_s, gvc) + b_s.T[None, None, :, :, None]
    return u * mixed.reshape(B, L, GM_WIDTH)


def hyena_filter(L, w1, b1, f1, w2, b2, f2, w3):
    t = jnp.linspace(0.0, 1.0, L, dtype=F32)[:, None]
    ang = 2.0 * math.pi * jnp.arange(L, dtype=F32)[:, None] / L
    bands = jnp.linspace(1e-4, HY_BANDS - 1, HY_BANDS, dtype=F32)[None, :]
    z = jnp.concatenate([t, jnp.cos(bands * ang), jnp.sin(-bands * ang)], axis=-1)
    hid = jnp.sin(f1 * (z @ w1 + b1))
    hid = jnp.sin(f2 * (hid @ w2 + b2))
    filt = hid @ w3
    deltas = jnp.abs(jnp.linspace(math.log(HY_DECAY_TARGET) / HY_SLOW_PCT,
                                  math.log(HY_DECAY_TARGET) / HY_FAST_PCT, HY_WIDTH, dtype=F32))
    decay = jnp.exp(-t * deltas[None, :])
    fwd = filt[:, :HY_WIDTH] * decay
    bwd = filt[:, HY_WIDTH:] * decay
    l1 = jnp.sum(jnp.abs(fwd), axis=0) + jnp.sum(jnp.abs(bwd[1:]), axis=0)
    k2 = jnp.concatenate([fwd, jnp.zeros((1, HY_WIDTH), F32), bwd[:0:-1]], axis=0)
    return k2 / l1[None, :]


def long_conv(v, k2):
    L = v.shape[1]
    vf = jnp.fft.rfft(v, n=2 * L, axis=1)
    kf = jnp.fft.rfft(k2, n=2 * L, axis=0)
    return jnp.fft.irfft(vf * kf[None], n=2 * L, axis=1)[:, :L]


def hyena(x, sc, sh, w_in, b_in, conv_w, conv_b, w1, b1, f1, w2, b2, f2, w3, d_skip, w_out):
    L = x.shape[1]
    z = mod_matmul(x, sc, sh, w_in) + b_in
    z = lax.conv_general_dilated(z, conv_w[:, None, :], window_strides=(1,), padding=[(1, 1)],
                                 dimension_numbers=('NWC', 'WIO', 'NWC'),
                                 feature_group_count=3 * HY_WIDTH) + conv_b
    x0, x1, v = jnp.split(z, 3, axis=-1)
    k2 = hyena_filter(L, w1, b1, f1, w2, b2, f2, w3)
    v = v * x1
    y = (long_conv(v, k2) + v * d_skip) * x0
    return matmul(y, w_out)


TOPK_TT = 256


def _top_rows(s, row_id, k):
    n_rows = float(s.shape[0])
    vals, ids = [], []
    for _ in range(k):
        m = jnp.max(s, axis=0, keepdims=True)
        sel = jnp.min(jnp.where(s == m, row_id, n_rows), axis=0, keepdims=True)
        vals.append(m)
        ids.append(sel)
        s = jnp.where(row_id == sel, -jnp.inf, s)
    return jnp.concatenate(vals, axis=0), jnp.concatenate(ids, axis=0)


def _peer_topk_kernel(q_ref, keys_ref, idx_ref, gate_ref):
    K, NK, half = PEER_TOPK, PEER_NKEYS, PEER_DK // 2
    nt = (((1,), (1,)), ((), ()))
    q = q_ref[...].astype(BF16)
    key_id = lax.broadcasted_iota(jnp.int32, (NK, q.shape[0]), 0).astype(F32)
    parts = []
    for p in range(2):
        s = lax.dot_general(keys_ref[0, p].astype(BF16), q[:, p * half:(p + 1) * half], nt,
                            preferred_element_type=F32)
        parts.append(_top_rows(s, key_id, K))
    (s1, i1), (s2, i2) = parts
    cand_s = jnp.concatenate([s1[i:i + 1] + s2 for i in range(K)], axis=0)
    cand_e = jnp.concatenate([i1[i:i + 1] * float(NK) + i2 for i in range(K)], axis=0)
    cand_id = lax.broadcasted_iota(jnp.int32, cand_s.shape, 0).astype(F32)
    vals, ids = [], []
    for _ in range(K):
        m = jnp.max(cand_s, axis=0, keepdims=True)
        sel = jnp.min(jnp.where(cand_s == m, cand_id, float(K * K)), axis=0, keepdims=True)
        hit = cand_id == sel
        vals.append(m)
        ids.append(jnp.max(jnp.where(hit, cand_e, -1.0), axis=0, keepdims=True))
        cand_s = jnp.where(hit, -jnp.inf, cand_s)
    top_s = jnp.concatenate(vals, axis=0)
    e = jnp.exp(top_s - top_s[0:1])
    gate_ref[...] = e / jnp.sum(e, axis=0, keepdims=True)
    idx_ref[...] = jnp.concatenate(ids, axis=0).astype(jnp.int32)


def peer_topk(q, keys):
    N = q.shape[0]
    return pl.pallas_call(
        _peer_topk_kernel,
        grid=(N // TOPK_TT, PEER_HEADS),
        in_specs=[
            pl.BlockSpec((TOPK_TT, PEER_DK), lambda i, h: (i, h)),
            pl.BlockSpec((1, 2, PEER_NKEYS, PEER_DK // 2), lambda i, h: (h, 0, 0, 0)),
        ],
        out_specs=[pl.BlockSpec((PEER_TOPK, TOPK_TT), lambda i, h: (h, i)),
                   pl.BlockSpec((PEER_TOPK, TOPK_TT), lambda i, h: (h, i))],
        out_shape=[jax.ShapeDtypeStruct((PEER_NSEL, N), jnp.int32), jax.ShapeDtypeStruct((PEER_NSEL, N), F32)],
        compiler_params=pltpu.CompilerParams(
            dimension_semantics=("arbitrary", "arbitrary"), vmem_limit_bytes=VMEM_LIMIT),
        name="peer_topk",
    )(q, keys)


def peer_select(x, sc, sh, wq, keys):
    B, L, D = x.shape
    q = mod_matmul(x, sc, sh, wq).reshape(B * L, PEER_HEADS * PEER_DK)
    return peer_topk(q, keys)


def kernel(x, c, ctx, c_ctx, mod_w, mod_b, ln_g, ln_b, ab_w_in, ab_w_out, na_rpb, gm_w_s, gm_b_s, hy_w_in, hy_b_in, hy_conv_w, hy_conv_b, hy_ffn_w1, hy_ffn_b1, hy_ffn_f1, hy_ffn_w2, hy_ffn_b2, hy_ffn_f2, hy_ffn_w3, hy_d, hy_w_out, peer_wq, peer_keys, peer_u, peer_v):
    for i in range(DEPTH):
        j = i // 2
        mod = (jax.nn.silu(c) @ mod_w[i] + mod_b[i])[:, None, :]
        sh1, sc1, g1, sh2, sc2, g2 = jnp.split(mod, 6, axis=-1)
        if i % 2 == 0:
            mod_c = (jax.nn.silu(c_ctx[None, :]) @ mod_w[i] + mod_b[i])[:, None, :]
            csh1, csc1 = mod_c[..., :D_MODEL], mod_c[..., D_MODEL:2 * D_MODEL]
            p = mod_matmul(x, sc1, sh1, ab_w_in[j])
            u, gv = jnp.split(jax.nn.gelu(p[..., AB_QKV:], approximate=False), 2, axis=-1)
            B = x.shape[0]
            pc = mod_matmul(ctx, jnp.broadcast_to(csc1, (B, 1, D_MODEL)), jnp.broadcast_to(csh1, (B, 1, D_MODEL)),
                            ab_w_in[j][:, NA_DIM:AB_QKV], tm=256)
            a = neighbourhood_attention(p, pc, na_rpb[j])
            g = chunk_spatial_gating(u, gv, gm_w_s[j], gm_b_s[j])
            y = matmul(jnp.concatenate([a, g], axis=-1), ab_w_out[j])
        else:
            y = hyena(x, sc1, sh1, hy_w_in[j], hy_b_in[j], hy_conv_w[j], hy_conv_b[j], hy_ffn_w1[j], hy_ffn_b1[j],
                      hy_ffn_f1[j], hy_ffn_w2[j], hy_ffn_b2[j], hy_ffn_f2[j], hy_ffn_w3[j], hy_d[j], hy_w_out[j])
        x = layer_norm(DN_ALPHA * x + g1 * y, ln_g[i, 0], ln_b[i, 0])
        idx, gate = peer_select(x, sc2, sh2, peer_wq[i], peer_keys[i])
        x = peer_apply(x, sc2, sh2, g2, ln_g[i, 1], ln_b[i, 1], idx, gate, peer_u[i], peer_v[i])
    return x
```

```python
import math

import jax
import jax.numpy as jnp
from jax import lax
from jax.experimental import pallas as pl
from jax.experimental.pallas import tpu as pltpu

D_MODEL = 1024
DEPTH = 2
GRID_W = 64
NA_HEADS = 8
NA_HEAD_DIM = 64
NA_DIM = NA_HEADS * NA_HEAD_DIM
NA_WIN_ROWS = 8
NA_WIN_COLS = 16
GM_GROUPS = 4
GM_GROUP_DIM = 128
GM_WIDTH = GM_GROUPS * GM_GROUP_DIM
GM_CHUNK = 128
AB_QKV = 3 * NA_DIM
HY_WIDTH = D_MODEL
HY_BANDS = 16
HY_DECAY_TARGET = 1e-2
HY_FAST_PCT = 0.3
HY_SLOW_PCT = 1.5
PEER_HEADS = 8
PEER_NKEYS = 128
PEER_DK = 256
PEER_TOPK = 16
DN_ALPHA = (2 * DEPTH) ** 0.25
LN_EPS = 1e-5
F32 = jnp.float32
BF16 = jnp.bfloat16

VMEM_LIMIT = 56 * 1024 * 1024


def _mod_mm_kernel(x_ref, sc_ref, sh_ref, w_ref, o_ref):
    h = x_ref[0] * (1.0 + sc_ref[0]) + sh_ref[0]
    o_ref[0] = jnp.dot(h.astype(BF16), w_ref[...], preferred_element_type=F32)


def mod_matmul(x, scale, shift, w, tm=512):
    B, L, K = x.shape
    N = w.shape[1]
    return pl.pallas_call(
        _mod_mm_kernel,
        grid=(B, L // tm),
        in_specs=[
            pl.BlockSpec((1, tm, K), lambda b, i: (b, i, 0)),
            pl.BlockSpec((1, 1, K), lambda b, i: (b, 0, 0)),
            pl.BlockSpec((1, 1, K), lambda b, i: (b, 0, 0)),
            pl.BlockSpec((K, N), lambda b, i: (0, 0)),
        ],
        out_specs=pl.BlockSpec((1, tm, N), lambda b, i: (b, i, 0)),
        out_shape=jax.ShapeDtypeStruct((B, L, N), F32),
        compiler_params=pltpu.CompilerParams(
            dimension_semantics=("arbitrary", "arbitrary"), vmem_limit_bytes=VMEM_LIMIT),
        name="mod_matmul",
    )(x, scale, shift, w.astype(BF16))


def matmul(x, w, tm=512):
    B, L, K = x.shape
    zeros = jnp.zeros((B, 1, K), F32)
    return mod_matmul(x, zeros, zeros, w, tm=tm)


def normalize(x):
    mu = jnp.mean(x, axis=-1, keepdims=True)
    var = jnp.mean(jnp.square(x - mu), axis=-1, keepdims=True)
    return (x - mu) * lax.rsqrt(var + LN_EPS)


def layer_norm(x, g, b):
    return normalize(x) * g + b


NA_RB = 8
NA_NEG = -1e30


def _na_kernel(q_ref, ka_ref, kb_ref, kc_ref, va_ref, vb_ref, vc_ref, kx_ref, vx_ref, bias_ref, o_ref,
               kbuf, vbuf):
    j = pl.program_id(1)
    W, H, dh, WR = GRID_W, NA_HEADS, NA_HEAD_DIM, NA_WIN_ROWS
    nb = W * NA_RB
    rows = pl.num_programs(1) * NA_RB
    for s, (kr, vr) in enumerate(((ka_ref, va_ref), (kb_ref, vb_ref), (kc_ref, vc_ref))):
        kbuf[s * nb:(s + 1) * nb, :] = kr[0].astype(BF16)
        vbuf[s * nb:(s + 1) * nb, :] = vr[0].astype(BF16)
    kx = kx_ref[0].astype(BF16)
    vx = vx_ref[0].astype(BF16)
    row_head = lax.broadcasted_iota(jnp.int32, (H * W, H * dh), 0) // W
    col_head = lax.broadcasted_iota(jnp.int32, (H * W, H * dh), 1) // dh
    head_mask = row_head == col_head
    nt = (((1,), (1,)), ((), ()))

    def row_body(rl, carry):
        r = j * NA_RB + rl
        rs = jnp.clip(r - WR // 2, 0, rows - WR)
        off = pl.multiple_of((rs - (j - 1) * NA_RB) * W, W)
        d0 = rs - r + (WR - 1)
        q_r = q_ref[0, pl.ds(pl.multiple_of(rl * W, W), W), :] * (dh ** -0.5)
        q_st = jnp.where(head_mask, jnp.concatenate([q_r] * H, axis=0), 0.0).astype(BF16)
        k_win = kbuf[pl.ds(off, WR * W), :]
        v_win = vbuf[pl.ds(off, WR * W), :]
        s_loc = lax.dot_general(q_st, k_win, nt, preferred_element_type=F32) + bias_ref[d0]
        s_ctx = lax.dot_general(q_st, kx, nt, preferred_element_type=F32)
        m = jnp.maximum(jnp.max(s_loc, axis=-1, keepdims=True), jnp.max(s_ctx, axis=-1, keepdims=True))
        p_loc = jnp.exp(s_loc - m)
        p_ctx = jnp.exp(s_ctx - m)
        den = jnp.sum(p_loc, axis=-1, keepdims=True) + jnp.sum(p_ctx, axis=-1, keepdims=True)
        o_all = (jnp.dot(p_loc.astype(BF16), v_win, preferred_element_type=F32)
                 + jnp.dot(p_ctx.astype(BF16), vx, preferred_element_type=F32)) / den
        o_all = jnp.where(head_mask, o_all, 0.0)
        out = o_all[0:W]
        for h in range(1, H):
            out = out + o_all[h * W:(h + 1) * W]
        o_ref[0, pl.ds(pl.multiple_of(rl * W, W), W), :] = out
        return carry

    lax.fori_loop(0, NA_RB, row_body, 0)


def _na_bias_table(rpb):
    W, H, WR, WC = GRID_W, NA_HEADS, NA_WIN_ROWS, NA_WIN_COLS
    qc = jnp.arange(W)
    cs = jnp.clip(qc - WC // 2, 0, W - WC)
    kc = jnp.arange(W)
    valid = (kc[None, :] >= cs[:, None]) & (kc[None, :] < cs[:, None] + WC)
    crel = jnp.clip(kc[None, :] - qc[:, None] + (WC - 1), 0, 2 * WC - 2)
    rr = jnp.arange(WR)[:, None] + jnp.arange(WR)[None, :]
    t = rpb[:, rr]
    t = t[:, :, :, crel]
    t = jnp.where(valid[None, None, None], t, NA_NEG)
    return t.transpose(1, 0, 3, 2, 4).reshape(WR, H * W, WR * W).astype(F32)


def neighbourhood_attention(p, pc, rpb):
    B, L, _ = p.shape
    Lc = pc.shape[1]
    nb = GRID_W * NA_RB
    nj = L // nb
    blk = (1, nb, NA_DIM)
    return pl.pallas_call(
        _na_kernel,
        grid=(B, nj),
        in_specs=[
            pl.BlockSpec(blk, lambda b, j: (b, j, 0)),
            pl.BlockSpec(blk, lambda b, j: (b, jnp.maximum(j - 1, 0), 1)),
            pl.BlockSpec(blk, lambda b, j: (b, j, 1)),
            pl.BlockSpec(blk, lambda b, j: (b, jnp.minimum(j + 1, nj - 1), 1)),
            pl.BlockSpec(blk, lambda b, j: (b, jnp.maximum(j - 1, 0), 2)),
            pl.BlockSpec(blk, lambda b, j: (b, j, 2)),
            pl.BlockSpec(blk, lambda b, j: (b, jnp.minimum(j + 1, nj - 1), 2)),
            pl.BlockSpec((1, Lc, NA_DIM), lambda b, j: (b, 0, 0)),
            pl.BlockSpec((1, Lc, NA_DIM), lambda b, j: (b, 0, 1)),
            pl.BlockSpec((NA_WIN_ROWS, NA_HEADS * GRID_W, NA_WIN_ROWS * GRID_W), lambda b, j: (0, 0, 0)),
        ],
        out_specs=pl.BlockSpec(blk, lambda b, j: (b, j, 0)),
        out_shape=jax.ShapeDtypeStruct((B, L, NA_DIM), F32),
        scratch_shapes=[pltpu.VMEM((3 * nb, NA_DIM), BF16), pltpu.VMEM((3 * nb, NA_DIM), BF16)],
        compiler_params=pltpu.CompilerParams(
            dimension_semantics=("arbitrary", "arbitrary"), vmem_limit_bytes=VMEM_LIMIT),
        name="na_attention",
    )(p, p, p, p, p, p, p, pc, pc, _na_bias_table(rpb))


PEER_TB = 8
PEER_NSEL = PEER_HEADS * PEER_TOPK
PEER_NBUF = 4
PEER_TS = PEER_TB * PEER_NBUF


def _gelu_exact(x):
    return 0.5 * x * (1.0 + lax.erf(x * (2.0 ** -0.5)))


def _peer_kernel(idc_ref, idn_ref, x_ref, sc_ref, sh_ref, g2_ref, gate_ref, lng_ref, lnb_ref, uv_hbm,
                 o_ref, buf, sem):
    g = pl.program_id(0)
    n = pl.num_programs(0)
    D = x_ref.shape[-1]
    ahead = PEER_NBUF - 1
    half = PEER_NSEL // 2

    def issue(ids_ref, src_blk, dst_slot, t, k0, k1):
        for k in range(k0, k1):
            e = ids_ref[0, 0, (src_blk * PEER_TB + t) * PEER_NSEL + k]
            pltpu.make_async_copy(uv_hbm.at[e], buf.at[dst_slot, t, pl.ds(k, 1)],
                                  sem.at[dst_slot]).start(priority=k % 2)

    def wait_slot(s):
        pltpu.make_async_copy(buf.at[s], buf.at[s], sem.at[s]).wait()

    @pl.when(g == 0)
    def _():
        for s in range(ahead):
            for t in range(PEER_TB):
                issue(idc_ref, s, s, t, 0, PEER_NSEL)

    x = x_ref[...]
    xm = x * (1.0 + sc_ref[0]) + sh_ref[0]
    gate = gate_ref[0]
    outs = []
    for s in range(PEER_NBUF):
        wait_slot(s)
        nb = s + ahead
        ids_ref, src_blk = (idc_ref, nb) if nb < PEER_NBUF else (idn_ref, nb - PEER_NBUF)
        dst = nb % PEER_NBUF
        acts = []
        for t in range(PEER_TB):
            issue(ids_ref, src_blk, dst, t, 0, half)
            row = s * PEER_TB + t
            u = lax.bitcast_convert_type(buf[s, t] << 16, F32)
            acts.append(jnp.sum(u * xm[row:row + 1, :], axis=-1, keepdims=True))
        w = gate[:, s * PEER_TB:(s + 1) * PEER_TB] * _gelu_exact(jnp.concatenate(acts, axis=1))
        for t in range(PEER_TB):
            issue(ids_ref, src_blk, dst, t, half, PEER_NSEL)
            v = lax.bitcast_convert_type(buf[s, t] & jnp.uint32(0xFFFF0000), F32)
            outs.append(jnp.sum(w[:, t:t + 1] * v, axis=0, keepdims=True))

    @pl.when(g == n - 1)
    def _():
        for s in range(ahead):
            wait_slot(s)

    y = DN_ALPHA * x + g2_ref[0] * jnp.concatenate(outs, axis=0)
    mu = jnp.mean(y, axis=-1, keepdims=True)
    yc = y - mu
    var = jnp.mean(yc * yc, axis=-1, keepdims=True)
    o_ref[...] = yc * lax.rsqrt(var + LN_EPS) * lng_ref[...] + lnb_ref[...]


def peer_apply(x, sc, sh, g2, lng, lnb, idx_t, gate_t, u_tab, v_tab):
    B, L, D = x.shape
    N = B * L
    nstep = N // PEER_TS
    nid = PEER_TS * PEER_NSEL
    u16 = lax.bitcast_convert_type(u_tab.astype(BF16), jnp.uint16).astype(jnp.uint32)
    v16 = lax.bitcast_convert_type(v_tab.astype(BF16), jnp.uint16).astype(jnp.uint32)
    uv = (u16 | (v16 << 16))[:, None, :]
    ids = idx_t.T.reshape(nstep, 1, nid)
    gate_b = gate_t.reshape(PEER_NSEL, nstep, PEER_TS).transpose(1, 0, 2)
    per_b = L // PEER_TS
    vec = lambda i: (i // per_b, 0, 0)
    out = pl.pallas_call(
        _peer_kernel,
        grid=(nstep,),
        in_specs=[
            pl.BlockSpec((1, 1, nid), lambda i: (i, 0, 0), memory_space=pltpu.SMEM),
            pl.BlockSpec((1, 1, nid), lambda i: (jnp.minimum(i + 1, nstep - 1), 0, 0), memory_space=pltpu.SMEM),
            pl.BlockSpec((PEER_TS, D), lambda i: (i, 0)),
            pl.BlockSpec((1, 1, D), vec),
            pl.BlockSpec((1, 1, D), vec),
            pl.BlockSpec((1, 1, D), vec),
            pl.BlockSpec((1, PEER_NSEL, PEER_TS), lambda i: (i, 0, 0)),
            pl.BlockSpec((1, D), lambda i: (0, 0)),
            pl.BlockSpec((1, D), lambda i: (0, 0)),
            pl.BlockSpec(memory_space=pl.ANY),
        ],
        out_specs=pl.BlockSpec((PEER_TS, D), lambda i: (i, 0)),
        out_shape=jax.ShapeDtypeStruct((N, D), F32),
        scratch_shapes=[pltpu.VMEM((PEER_NBUF, PEER_TB, PEER_NSEL, D), jnp.uint32),
                        pltpu.SemaphoreType.DMA((PEER_NBUF,))],
        compiler_params=pltpu.CompilerParams(
            dimension_semantics=("arbitrary",), vmem_limit_bytes=VMEM_LIMIT, disable_bounds_checks=True),
        name="peer_gather",
    )(ids, ids, x.reshape(N, D), sc, sh, g2, gate_b, lng[None, :], lnb[None, :], uv)
    return out.reshape(B, L, D)


def chunk_spatial_gating(u, gv, w_s, b_s):
    B, L, _ = u.shape
    n = L // GM_CHUNK
    gvc = normalize(gv).reshape(B, n, GM_CHUNK, GM_GROUPS, GM_GROUP_DIM)
    mixed = jnp.einsum('gpq,bnqgc->bnpgc', w_s, gvc) + b_s.T[None, None, :, :, None]
    return u * mixed.reshape(B, L, GM_WIDTH)


def hyena_filter(L, w1, b1, f1, w2, b2, f2, w3):
    t = jnp.linspace(0.0, 1.0, L, dtype=F32)[:, None]
    ang = 2.0 * math.pi * jnp.arange(L, dtype=F32)[:, None] / L
    bands = jnp.linspace(1e-4, HY_BANDS - 1, HY_BANDS, dtype=F32)[None, :]
    z = jnp.concatenate([t, jnp.cos(bands * ang), jnp.sin(-bands * ang)], axis=-1)
    hid = jnp.sin(f1 * (z @ w1 + b1))
    hid = jnp.sin(f2 * (hid @ w2 + b2))
    filt = hid @ w3
    deltas = jnp.abs(jnp.linspace(math.log(HY_DECAY_TARGET) / HY_SLOW_PCT,
                                  math.log(HY_DECAY_TARGET) / HY_FAST_PCT, HY_WIDTH, dtype=F32))
    decay = jnp.exp(-t * deltas[None, :])
    fwd = filt[:, :HY_WIDTH] * decay
    bwd = filt[:, HY_WIDTH:] * decay
    l1 = jnp.sum(jnp.abs(fwd), axis=0) + jnp.sum(jnp.abs(bwd[1:]), axis=0)
    k2 = jnp.concatenate([fwd, jnp.zeros((1, HY_WIDTH), F32), bwd[:0:-1]], axis=0)
    return k2 / l1[None, :]


def long_conv(v, k2):
    L = v.shape[1]
    vf = jnp.fft.rfft(v, n=2 * L, axis=1)
    kf = jnp.fft.rfft(k2, n=2 * L, axis=0)
    return jnp.fft.irfft(vf * kf[None], n=2 * L, axis=1)[:, :L]


def hyena(x, sc, sh, w_in, b_in, conv_w, conv_b, w1, b1, f1, w2, b2, f2, w3, d_skip, w_out):
    L = x.shape[1]
    z = mod_matmul(x, sc, sh, w_in) + b_in
    z = lax.conv_general_dilated(z, conv_w[:, None, :], window_strides=(1,), padding=[(1, 1)],
                                 dimension_numbers=('NWC', 'WIO', 'NWC'),
                                 feature_group_count=3 * HY_WIDTH) + conv_b
    x0, x1, v = jnp.split(z, 3, axis=-1)
    k2 = hyena_filter(L, w1, b1, f1, w2, b2, f2, w3)
    v = v * x1
    y = (long_conv(v, k2) + v * d_skip) * x0
    return matmul(y, w_out)


TOPK_TT = 256


def _top_rows(s, row_id, k):
    n_rows = float(s.shape[0])
    vals, ids = [], []
    for _ in range(k):
        m = jnp.max(s, axis=0, keepdims=True)
        sel = jnp.min(jnp.where(s == m, row_id, n_rows), axis=0, keepdims=True)
        vals.append(m)
        ids.append(sel)
        s = jnp.where(row_id == sel, -jnp.inf, s)
    return jnp.concatenate(vals, axis=0), jnp.concatenate(ids, axis=0)


def _peer_topk_kernel(q_ref, keys_ref, idx_ref, gate_ref):
    K, NK, half = PEER_TOPK, PEER_NKEYS, PEER_DK // 2
    nt = (((1,), (1,)), ((), ()))
    q = q_ref[...].astype(BF16)
    key_id = lax.broadcasted_iota(jnp.int32, (NK, q.shape[0]), 0).astype(F32)
    parts = []
    for p in range(2):
        s = lax.dot_general(keys_ref[0, p].astype(BF16), q[:, p * half:(p + 1) * half], nt,
                            preferred_element_type=F32)
        parts.append(_top_rows(s, key_id, K))
    (s1, i1), (s2, i2) = parts
    cand_s = jnp.concatenate([s1[i:i + 1] + s2 for i in range(K)], axis=0)
    cand_e = jnp.concatenate([i1[i:i + 1] * float(NK) + i2 for i in range(K)], axis=0)
    cand_id = lax.broadcasted_iota(jnp.int32, cand_s.shape, 0).astype(F32)
    vals, ids = [], []
    for _ in range(K):
        m = jnp.max(cand_s, axis=0, keepdims=True)
        sel = jnp.min(jnp.where(cand_s == m, cand_id, float(K * K)), axis=0, keepdims=True)
        hit = cand_id == sel
        vals.append(m)
        ids.append(jnp.max(jnp.where(hit, cand_e, -1.0), axis=0, keepdims=True))
        cand_s = jnp.where(hit, -jnp.inf, cand_s)
    top_s = jnp.concatenate(vals, axis=0)
    e = jnp.exp(top_s - top_s[0:1])
    gate_ref[...] = e / jnp.sum(e, axis=0, keepdims=True)
    idx_ref[...] = jnp.concatenate(ids, axis=0).astype(jnp.int32)


def peer_topk(q, keys):
    N = q.shape[0]
    return pl.pallas_call(
        _peer_topk_kernel,
        grid=(N // TOPK_TT, PEER_HEADS),
        in_specs=[
            pl.BlockSpec((TOPK_TT, PEER_DK), lambda i, h: (i, h)),
            pl.BlockSpec((1, 2, PEER_NKEYS, PEER_DK // 2), lambda i, h: (h, 0, 0, 0)),
        ],
        out_specs=[pl.BlockSpec((PEER_TOPK, TOPK_TT), lambda i, h: (h, i)),
                   pl.BlockSpec((PEER_TOPK, TOPK_TT), lambda i, h: (h, i))],
        out_shape=[jax.ShapeDtypeStruct((PEER_NSEL, N), jnp.int32), jax.ShapeDtypeStruct((PEER_NSEL, N), F32)],
        compiler_params=pltpu.CompilerParams(
            dimension_semantics=("arbitrary", "arbitrary"), vmem_limit_bytes=VMEM_LIMIT),
        name="peer_topk",
    )(q, keys)


def peer_select(x, sc, sh, wq, keys):
    B, L, D = x.shape
    q = mod_matmul(x, sc, sh, wq).reshape(B * L, PEER_HEADS * PEER_DK)
    return peer_topk(q, keys)


def kernel(x, c, ctx, c_ctx, mod_w, mod_b, ln_g, ln_b, ab_w_in, ab_w_out, na_rpb, gm_w_s, gm_b_s, hy_w_in, hy_b_in, hy_conv_w, hy_conv_b, hy_ffn_w1, hy_ffn_b1, hy_ffn_f1, hy_ffn_w2, hy_ffn_b2, hy_ffn_f2, hy_ffn_w3, hy_d, hy_w_out, peer_wq, peer_keys, peer_u, peer_v):
    for i in range(DEPTH):
        j = i // 2
        mod = (jax.nn.silu(c) @ mod_w[i] + mod_b[i])[:, None, :]
        sh1, sc1, g1, sh2, sc2, g2 = jnp.split(mod, 6, axis=-1)
        if i % 2 == 0:
            mod_c = (jax.nn.silu(c_ctx[None, :]) @ mod_w[i] + mod_b[i])[:, None, :]
            csh1, csc1 = mod_c[..., :D_MODEL], mod_c[..., D_MODEL:2 * D_MODEL]
            p = mod_matmul(x, sc1, sh1, ab_w_in[j])
            u, gv = jnp.split(jax.nn.gelu(p[..., AB_QKV:], approximate=False), 2, axis=-1)
            B = x.shape[0]
            pc = mod_matmul(ctx, jnp.broadcast_to(csc1, (B, 1, D_MODEL)), jnp.broadcast_to(csh1, (B, 1, D_MODEL)),
                            ab_w_in[j][:, NA_DIM:AB_QKV], tm=256)
            a = neighbourhood_attention(p, pc, na_rpb[j])
            g = chunk_spatial_gating(u, gv, gm_w_s[j], gm_b_s[j])
            y = matmul(jnp.concatenate([a, g], axis=-1), ab_w_out[j])
        else:
            y = hyena(x, sc1, sh1, hy_w_in[j], hy_b_in[j], hy_conv_w[j], hy_conv_b[j], hy_ffn_w1[j], hy_ffn_b1[j],
                      hy_ffn_f1[j], hy_ffn_w2[j], hy_ffn_b2[j], hy_ffn_f2[j], hy_ffn_w3[j], hy_d[j], hy_w_out[j])
        x = layer_norm(DN_ALPHA * x + g1 * y, ln_g[i, 0], ln_b[i, 0])
        idx, gate = peer_select(x, sc2, sh2, peer_wq[i], peer_keys[i])
        x = peer_apply(x, sc2, sh2, g2, ln_g[i, 1], ln_b[i, 1], idx, gate, peer_u[i], peer_v[i])
    return x
```
